```python
import jax, jax.numpy as jnp
from jax import lax
import numpy as np

D_MODEL = 1024
BATCH = 8
SEQ = 2048
DEPTH = 1
DEC_BATCH = 16
DEC_SEQ = 64
PAST_LEN = 2048

CHUNK = 64
D_A = D_MODEL
D_B = D_MODEL
CONV_A_WIDTH = 3
CONV_B_WIDTH = 31
D_FF = 2816
D_IN = 3 * D_A + 2 * D_B + 2 * D_MODEL
EPS = 1e-6

kernel_name = "hybrid_streaming_conv_encoder_step"


def rms_norm(x, g):
    xf = x.astype(jnp.float32)
    y = xf * lax.rsqrt(jnp.mean(xf * xf, axis=-1, keepdims=True) + EPS)
    return (y * g.astype(jnp.float32)).astype(x.dtype)


def layer_norm(x, g, b):
    xf = x.astype(jnp.float32)
    mu = jnp.mean(xf, axis=-1, keepdims=True)
    var = jnp.mean(jnp.square(xf - mu), axis=-1, keepdims=True)
    y = (xf - mu) * lax.rsqrt(var + EPS)
    return (y * g.astype(jnp.float32) + b.astype(jnp.float32)).astype(x.dtype)


def swiglu_ffn(x, w_gate_up, w_down):
    gate, up = jnp.split(x @ w_gate_up, 2, axis=-1)
    return (jax.nn.silu(gate) * up) @ w_down


def causal_depthwise_conv(x, hist, w):
    width = w.shape[0]
    xp = jnp.concatenate([hist.astype(x.dtype), x], axis=1)
    y = lax.conv_general_dilated(
        xp, w[:, None, :].astype(x.dtype), window_strides=(1,), padding="VALID",
        dimension_numbers=("NWC", "WIO", "NWC"), feature_group_count=x.shape[-1])
    new_hist = xp[:, xp.shape[1] - (width - 1):]
    return y, new_hist


def encoder_layer(x, hist_a, hist_b, ffn1_norm, ffn1_w_gate_up, ffn1_w_down, mix_norm, w_in,
                  conv_a_w, conv_b_w, conv_b_bias, conv_b_ln_g, conv_b_ln_b,
                  w_a_out, w_b_out, w_out, ffn2_norm, ffn2_w_gate_up, ffn2_w_down):
    x = x + 0.5 * swiglu_ffn(rms_norm(x, ffn1_norm), ffn1_w_gate_up, ffn1_w_down)
    h = rms_norm(x, mix_norm)
    p = h @ w_in
    cuts = [D_A, 2 * D_A, 3 * D_A, 3 * D_A + 2 * D_B, 3 * D_A + 2 * D_B + D_MODEL]
    a_b, a_c, a_v, b_u, gate_a, gate_b = jnp.split(p, cuts, axis=-1)
    a_conv, new_a = causal_depthwise_conv(a_c * a_v, hist_a, conv_a_w)
    y_a = (a_b * a_conv) @ w_a_out
    glu = b_u[..., :D_B] * jax.nn.sigmoid(b_u[..., D_B:])
    b_conv, new_b = causal_depthwise_conv(glu, hist_b, conv_b_w)
    b_act = jax.nn.silu(layer_norm(b_conv + conv_b_bias, conv_b_ln_g, conv_b_ln_b))
    y_b = b_act @ w_b_out
    merged = jax.nn.sigmoid(gate_a) * y_a + jax.nn.sigmoid(gate_b) * y_b
    x = x + merged @ w_out
    x = x + 0.5 * swiglu_ffn(rms_norm(x, ffn2_norm), ffn2_w_gate_up, ffn2_w_down)
    return x, new_a, new_b


def setup_inputs(seed: int = 0) -> dict:
    key = jax.random.key(seed)
    k = jax.random.split(key, 24)
    f32 = jnp.float32

    def nrm(kk, shape, scale):
        return jax.random.normal(kk, shape, f32) * scale

    def gain(kk, shape):
        return 1.0 + 0.02 * jax.random.normal(kk, shape, f32)

    L = DEPTH
    return {
        "x_prompt": nrm(k[0], (BATCH, SEQ, D_MODEL), 1.0),
        "x_sample": nrm(k[1], (DEC_BATCH, DEC_SEQ, D_MODEL), 1.0),
        "cache_conv_a": nrm(k[2], (L, DEC_BATCH, CONV_A_WIDTH - 1, D_A), 1.0),
        "cache_conv_b": nrm(k[3], (L, DEC_BATCH, CONV_B_WIDTH - 1, D_B), 1.0),
        "ffn1_norm": gain(k[4], (L, D_MODEL)),
        "ffn1_w_gate_up": nrm(k[5], (L, D_MODEL, 2 * D_FF), D_MODEL ** -0.5),
        "ffn1_w_down": nrm(k[6], (L, D_FF, D_MODEL), D_FF ** -0.5),
        "mix_norm": gain(k[7], (L, D_MODEL)),
        "w_in": nrm(k[8], (L, D_MODEL, D_IN), D_MODEL ** -0.5),
        "conv_a_w": nrm(k[9], (L, CONV_A_WIDTH, D_A), CONV_A_WIDTH ** -0.5),
        "conv_b_w": nrm(k[10], (L, CONV_B_WIDTH, D_B), CONV_B_WIDTH ** -0.5),
        "conv_b_bias": nrm(k[11], (L, D_B), 0.02),
        "conv_b_ln_g": gain(k[12], (L, D_B)),
        "conv_b_ln_b": nrm(k[13], (L, D_B), 0.02),
        "w_a_out": nrm(k[14], (L, D_A, D_MODEL), D_A ** -0.5),
        "w_b_out": nrm(k[15], (L, D_B, D_MODEL), D_B ** -0.5),
        "w_out": nrm(k[16], (L, D_MODEL, D_MODEL), D_MODEL ** -0.5),
        "ffn2_norm": gain(k[17], (L, D_MODEL)),
        "ffn2_w_gate_up": nrm(k[18], (L, D_MODEL, 2 * D_FF), D_MODEL ** -0.5),
        "ffn2_w_down": nrm(k[19], (L, D_FF, D_MODEL), D_FF ** -0.5),
        "final_norm": gain(k[20], (D_MODEL,)),
    }


def reference(x_prompt, x_sample, cache_conv_a, cache_conv_b, ffn1_norm, ffn1_w_gate_up,
              ffn1_w_down, mix_norm, w_in, conv_a_w, conv_b_w, conv_b_bias, conv_b_ln_g,
              conv_b_ln_b, w_a_out, w_b_out, w_out, ffn2_norm, ffn2_w_gate_up, ffn2_w_down,
              final_norm):
    xp, xs = x_prompt, x_sample
    pa, pb, sa, sb = [], [], [], []
    for l in range(DEPTH):
        w = (ffn1_norm[l], ffn1_w_gate_up[l], ffn1_w_down[l], mix_norm[l], w_in[l],
             conv_a_w[l], conv_b_w[l], conv_b_bias[l], conv_b_ln_g[l], conv_b_ln_b[l],
             w_a_out[l], w_b_out[l], w_out[l], ffn2_norm[l], ffn2_w_gate_up[l], ffn2_w_down[l])
        zero_a = jnp.zeros((xp.shape[0], CONV_A_WIDTH - 1, D_A), xp.dtype)
        zero_b = jnp.zeros((xp.shape[0], CONV_B_WIDTH - 1, D_B), xp.dtype)
        xp, hpa, hpb = encoder_layer(xp, zero_a, zero_b, *w)
        xs, hsa, hsb = encoder_layer(xs, cache_conv_a[l], cache_conv_b[l], *w)
        pa.append(hpa); pb.append(hpb); sa.append(hsa); sb.append(hsb)
    y_prompt = rms_norm(xp, final_norm)
    y_sample = rms_norm(xs, final_norm)
    new_conv_a_prompt = jnp.stack(pa, axis=0)
    new_conv_b_prompt = jnp.stack(pb, axis=0)
    new_conv_a_sample = jnp.stack(sa, axis=0)
    new_conv_b_sample = jnp.stack(sb, axis=0)
    return (y_prompt, y_sample, new_conv_a_prompt, new_conv_b_prompt, new_conv_a_sample, new_conv_b_sample)
```

```python
import functools

import jax
import jax.numpy as jnp
from jax import lax
from jax.experimental import pallas as pl
from jax.experimental.pallas import tpu as pltpu

D_MODEL = 1024
D_FF = 2816
CONV_A_WIDTH = 3
CONV_B_WIDTH = 31
EPS = 1e-6

F32 = jnp.float32
BF16 = jnp.bfloat16

TOKEN_TILE = 512
COL_CHUNK = 256
CONV_ROWS = 32
HIST_A_PAD = 8
HIST_B_PAD = 32
VMEM_LIMIT_BYTES = 52 * 1024 * 1024


def _rms_norm(x, g):
    y = x * lax.rsqrt(jnp.mean(x * x, axis=-1, keepdims=True) + EPS)
    return y * g


def _dot(a, b):
    return jnp.dot(a, b, preferred_element_type=F32)


def _resident(shape):
    return pl.BlockSpec(shape, lambda t: (0,) * len(shape), pipeline_mode=pl.Buffered(1))


def _ffn_kernel(x_ref, g_ref, wgu_ref, wd_ref, fin_ref, o_ref, h_scr, acc_scr, *, final_norm):
    h_scr[...] = _rms_norm(x_ref[...], g_ref[...]).astype(BF16)
    for j in range(D_FF // COL_CHUNK):
        lo = j * COL_CHUNK
        h = h_scr[...]
        gate = _dot(h, wgu_ref[:, lo:lo + COL_CHUNK])
        up = _dot(h, wgu_ref[:, D_FF + lo:D_FF + lo + COL_CHUNK])
        act = (jax.nn.silu(gate) * up).astype(BF16)
        part = _dot(act, wd_ref[lo:lo + COL_CHUNK, :])
        if j == 0:
            acc_scr[...] = part
        else:
            acc_scr[...] += part
    y = x_ref[...] + 0.5 * acc_scr[...]
    if final_norm:
        y = _rms_norm(y, fin_ref[...])
    o_ref[...] = y


def _ffn(x, norm_g, w_gate_up, w_down, final_g, *, final_norm):
    n = x.shape[0]
    tm = min(TOKEN_TILE, n)
    assert n % tm == 0
    row_spec = pl.BlockSpec((tm, D_MODEL), lambda t: (t, 0))
    return pl.pallas_call(
        functools.partial(_ffn_kernel, final_norm=final_norm),
        grid=(n // tm,),
        in_specs=[
            row_spec,
            _resident((1, D_MODEL)),
            _resident((D_MODEL, 2 * D_FF)),
            _resident((D_FF, D_MODEL)),
            _resident((1, D_MODEL)),
        ],
        out_specs=row_spec,
        out_shape=jax.ShapeDtypeStruct((n, D_MODEL), F32),
        scratch_shapes=[
            pltpu.VMEM((tm, D_MODEL), BF16),
            pltpu.VMEM((tm, D_MODEL), F32),
        ],
        compiler_params=pltpu.CompilerParams(
            dimension_semantics=("arbitrary",), vmem_limit_bytes=VMEM_LIMIT_BYTES),
        name="ffn_final" if final_norm else "ffn",
    )(x, norm_g, w_gate_up, w_down, final_g)


def _mixer_kernel(x_ref, ha_ref, hb_ref, g_ref, win_ref, caw_ref, cbw_ref, cbb_ref, lng_ref,
                  lnb_ref, wao_ref, wbo_ref, wo_ref,
                  o_ref, na_ref, nb_ref,
                  h_scr, ab_scr, cv_scr, glu_scr, za_scr, zb_scr, m_scr,
                  *, seg_len, n_seg, tiles_per_seq):
    d = D_MODEL
    tm = seg_len * n_seg
    stride_a = HIST_A_PAD + seg_len
    stride_b = HIST_B_PAD + seg_len
    na = CONV_A_WIDTH - 1
    nb = CONV_B_WIDTH - 1

    h_scr[...] = _rms_norm(x_ref[...], g_ref[...]).astype(BF16)

    for c in range(d // COL_CHUNK):
        lo = c * COL_CHUNK
        cols = slice(lo, lo + COL_CHUNK)
        h = h_scr[...]
        ab_scr[:, cols] = _dot(h, win_ref[:, lo:lo + COL_CHUNK])
        cv = _dot(h, win_ref[:, d + lo:d + lo + COL_CHUNK]) * _dot(
            h, win_ref[:, 2 * d + lo:2 * d + lo + COL_CHUNK])
        glu = _dot(h, win_ref[:, 3 * d + lo:3 * d + lo + COL_CHUNK]) * jax.nn.sigmoid(
            _dot(h, win_ref[:, 4 * d + lo:4 * d + lo + COL_CHUNK]))
        for s in range(n_seg):
            rows = slice(s * seg_len, (s + 1) * seg_len)
            cv_scr[s * stride_a + HIST_A_PAD:(s + 1) * stride_a, cols] = cv[rows]
            glu_scr[s * stride_b + HIST_B_PAD:(s + 1) * stride_b, cols] = glu[rows]

    def load_hist():
        for s in range(n_seg):
            cv_scr[s * stride_a + HIST_A_PAD - na:s * stride_a + HIST_A_PAD, :] = ha_ref[s]
            glu_scr[s * stride_b + HIST_B_PAD - nb:s * stride_b + HIST_B_PAD, :] = hb_ref[s]

    if tiles_per_seq == 1:
        load_hist()
    else:
        pl.when(pl.program_id(0) % tiles_per_seq == 0)(load_hist)

    for s in range(n_seg):
        na_ref[s] = cv_scr[(s + 1) * stride_a - na:(s + 1) * stride_a, :]
        nb_ref[s] = glu_scr[(s + 1) * stride_b - nb:(s + 1) * stride_b, :]

    chunks_per_seg = seg_len // CONV_ROWS

    def conv_step(i, carry):
        seg = i // chunks_per_seg
        within = (i % chunks_per_seg) * CONV_ROWS
        row0 = pl.multiple_of(i * CONV_ROWS, CONV_ROWS)
        base_a = pl.multiple_of(seg * stride_a + within, 8)
        base_b = pl.multiple_of(seg * stride_b + within, 8)
        win_a = cv_scr.at[pl.ds(base_a, HIST_A_PAD + CONV_ROWS)]
        win_b = glu_scr.at[pl.ds(base_b, HIST_B_PAD + CONV_ROWS)]

        acc = None
        for k in range(CONV_A_WIDTH):
            off = HIST_A_PAD - na + k
            term = caw_ref[k:k + 1, :] * win_a[off:off + CONV_ROWS, :]
            acc = term if acc is None else acc + term
        za_scr[pl.ds(row0, CONV_ROWS), :] = (ab_scr[pl.ds(row0, CONV_ROWS), :] * acc).astype(BF16)

        acc = None
        for k in range(CONV_B_WIDTH):
            off = HIST_B_PAD - nb + k
            term = cbw_ref[k:k + 1, :] * win_b[off:off + CONV_ROWS, :]
            acc = term if acc is None else acc + term
        v = acc + cbb_ref[...]
        mu = jnp.mean(v, axis=-1, keepdims=True)
        var = jnp.mean(jnp.square(v - mu), axis=-1, keepdims=True)
        y = (v - mu) * lax.rsqrt(var + EPS)
        y = y * lng_ref[...] + lnb_ref[...]
        zb_scr[pl.ds(row0, CONV_ROWS), :] = jax.nn.silu(y).astype(BF16)
        return carry

    lax.fori_loop(0, tm // CONV_ROWS, conv_step, 0)

    if tiles_per_seq > 1:
        cv_scr[HIST_A_PAD - na:HIST_A_PAD, :] = cv_scr[stride_a - na:stride_a, :]
        glu_scr[HIST_B_PAD - nb:HIST_B_PAD, :] = glu_scr[stride_b - nb:stride_b, :]

    for c in range(d // COL_CHUNK):
        lo = c * COL_CHUNK
        h = h_scr[...]
        y_a = _dot(za_scr[...], wao_ref[:, lo:lo + COL_CHUNK])
        y_b = _dot(zb_scr[...], wbo_ref[:, lo:lo + COL_CHUNK])
        g_a = _dot(h, win_ref[:, 5 * d + lo:5 * d + lo + COL_CHUNK])
        g_b = _dot(h, win_ref[:, 6 * d + lo:6 * d + lo + COL_CHUNK])
        merged = jax.nn.sigmoid(g_a) * y_a + jax.nn.sigmoid(g_b) * y_b
        m_scr[:, lo:lo + COL_CHUNK] = merged.astype(BF16)

    for c in range(d // COL_CHUNK):
        lo = c * COL_CHUNK
        o_ref[:, lo:lo + COL_CHUNK] = x_ref[:, lo:lo + COL_CHUNK] + _dot(
            m_scr[...], wo_ref[:, lo:lo + COL_CHUNK])


def _mixer(x, hist_a, hist_b, seq_len, norm_g, w_in, conv_a_w, conv_b_w, conv_b_bias, ln_g, ln_b,
           w_a_out, w_b_out, w_out):
    n = x.shape[0]
    batch = n // seq_len
    tm = min(TOKEN_TILE, n)
    seg_len = min(seq_len, tm)
    n_seg = tm // seg_len
    tiles_per_seq = seq_len // seg_len
    assert n % tm == 0 and tm % seg_len == 0 and seq_len % seg_len == 0
    assert seg_len % CONV_ROWS == 0 and seg_len >= CONV_B_WIDTH - 1
    assert n_seg == 1 or tiles_per_seq == 1
    na = CONV_A_WIDTH - 1
    nb = CONV_B_WIDTH - 1

    row_spec = pl.BlockSpec((tm, D_MODEL), lambda t: (t, 0))

    def seq_block(rows):
        return pl.BlockSpec((n_seg, rows, D_MODEL), lambda t: (t // tiles_per_seq, 0, 0))

    kernel = functools.partial(_mixer_kernel, seg_len=seg_len, n_seg=n_seg,
                               tiles_per_seq=tiles_per_seq)
    return pl.pallas_call(
        kernel,
        grid=(n // tm,),
        in_specs=[
            row_spec,
            seq_block(na),
            seq_block(nb),
            _resident((1, D_MODEL)),
            _resident(w_in.shape),
            _resident(conv_a_w.shape),
            _resident(conv_b_w.shape),
            _resident((1, D_MODEL)),
            _resident((1, D_MODEL)),
            _resident((1, D_MODEL)),
            _resident(w_a_out.shape),
            _resident(w_b_out.shape),
            _resident(w_out.shape),
        ],
        out_specs=[row_spec, seq_block(na), seq_block(nb)],
        out_shape=[
            jax.ShapeDtypeStruct((n, D_MODEL), F32),
            jax.ShapeDtypeStruct((batch, na, D_MODEL), F32),
            jax.ShapeDtypeStruct((batch, nb, D_MODEL), F32),
        ],
        scratch_shapes=[
            pltpu.VMEM((tm, D_MODEL), BF16),
            pltpu.VMEM((tm, D_MODEL), F32),
            pltpu.VMEM((n_seg * (HIST_A_PAD + seg_len), D_MODEL), F32),
            pltpu.VMEM((n_seg * (HIST_B_PAD + seg_len), D_MODEL), F32),
            pltpu.VMEM((tm, D_MODEL), BF16),
            pltpu.VMEM((tm, D_MODEL), BF16),
            pltpu.VMEM((tm, D_MODEL), BF16),
        ],
        compiler_params=pltpu.CompilerParams(
            dimension_semantics=("arbitrary",), vmem_limit_bytes=VMEM_LIMIT_BYTES),
        name="mixer",
    )(x, hist_a, hist_b, norm_g, w_in, conv_a_w, conv_b_w, conv_b_bias, ln_g, ln_b,
      w_a_out, w_b_out, w_out)


def kernel(x_prompt, x_sample, cache_conv_a, cache_conv_b, ffn1_norm, ffn1_w_gate_up, ffn1_w_down,
           mix_norm, w_in, conv_a_w, conv_b_w, conv_b_bias, conv_b_ln_g, conv_b_ln_b, w_a_out,
           w_b_out, w_out, ffn2_norm, ffn2_w_gate_up, ffn2_w_down, final_norm):
    depth = ffn1_norm.shape[0]
    batch_p, seq_p, _ = x_prompt.shape
    batch_s, seq_s, _ = x_sample.shape
    na = CONV_A_WIDTH - 1
    nb = CONV_B_WIDTH - 1

    xp = x_prompt.reshape(batch_p * seq_p, D_MODEL)
    xs = x_sample.reshape(batch_s * seq_s, D_MODEL)
    fin = final_norm.reshape(1, D_MODEL)
    zero_a = jnp.zeros((batch_p, na, D_MODEL), F32)
    zero_b = jnp.zeros((batch_p, nb, D_MODEL), F32)

    pa, pb, sa, sb = [], [], [], []
    for l in range(depth):
        last = l == depth - 1
        row = lambda v: v[l].reshape(1, D_MODEL)
        ffn1 = (row(ffn1_norm), ffn1_w_gate_up[l].astype(BF16), ffn1_w_down[l].astype(BF16), fin)
        ffn2 = (row(ffn2_norm), ffn2_w_gate_up[l].astype(BF16), ffn2_w_down[l].astype(BF16), fin)
        mix = (row(mix_norm), w_in[l].astype(BF16), conv_a_w[l], conv_b_w[l], row(conv_b_bias),
               row(conv_b_ln_g), row(conv_b_ln_b), w_a_out[l].astype(BF16),
               w_b_out[l].astype(BF16), w_out[l].astype(BF16))

        xp = _ffn(xp, *ffn1, final_norm=False)
        xp, hpa, hpb = _mixer(xp, zero_a, zero_b, seq_p, *mix)
        xp = _ffn(xp, *ffn2, final_norm=last)

        xs = _ffn(xs, *ffn1, final_norm=False)
        xs, hsa, hsb = _mixer(xs, cache_conv_a[l], cache_conv_b[l], seq_s, *mix)
        xs = _ffn(xs, *ffn2, final_norm=last)

        pa.append(hpa); pb.append(hpb); sa.append(hsa); sb.append(hsb)

    return (xp.reshape(x_prompt.shape), xs.reshape(x_sample.shape),
            jnp.stack(pa, axis=0), jnp.stack(pb, axis=0), jnp.stack(sa, axis=0), jnp.stack(sb, axis=0))
```

```python
import functools

import jax
import jax.numpy as jnp
from jax import lax
from jax.experimental import pallas as pl
from jax.experimental.pallas import tpu as pltpu

D_MODEL = 1024
D_FF = 2816
CONV_A_WIDTH = 3
CONV_B_WIDTH = 31
EPS = 1e-6

F32 = jnp.float32
BF16 = jnp.bfloat16

TOKEN_TILE = 512
COL_CHUNK = 256
LANES = 128
CONV_ROWS = 64
LN_ROWS = 16
HIST_A_PAD = 8
HIST_B_PAD = 32
VMEM_LIMIT_BYTES = 52 * 1024 * 1024


def _rms_norm(x, g):
    y = x * lax.rsqrt(jnp.mean(x * x, axis=-1, keepdims=True) + EPS)
    return y * g


def _dot(a, b):
    return jnp.dot(a, b, preferred_element_type=F32)


def _resident(shape):
    return pl.BlockSpec(shape, lambda t: (0,) * len(shape), pipeline_mode=pl.Buffered(1))


def _ffn_kernel(x_ref, g_ref, wgu_ref, wd_ref, fin_ref, o_ref, h_scr, acc_scr, *, final_norm):
    h_scr[...] = _rms_norm(x_ref[...], g_ref[...]).astype(BF16)
    for j in range(D_FF // COL_CHUNK):
        lo = j * COL_CHUNK
        h = h_scr[...]
        gate = _dot(h, wgu_ref[:, lo:lo + COL_CHUNK])
        up = _dot(h, wgu_ref[:, D_FF + lo:D_FF + lo + COL_CHUNK])
        act = (jax.nn.silu(gate) * up).astype(BF16)
        part = _dot(act, wd_ref[lo:lo + COL_CHUNK, :])
        if j == 0:
            acc_scr[...] = part
        else:
            acc_scr[...] += part
    y = x_ref[...] + 0.5 * acc_scr[...]
    if final_norm:
        y = _rms_norm(y, fin_ref[...])
    o_ref[...] = y


def _ffn(x, norm_g, w_gate_up, w_down, final_g, *, final_norm):
    n = x.shape[0]
    tm = min(TOKEN_TILE, n)
    assert n % tm == 0
    row_spec = pl.BlockSpec((tm, D_MODEL), lambda t: (t, 0))
    return pl.pallas_call(
        functools.partial(_ffn_kernel, final_norm=final_norm),
        grid=(n // tm,),
        in_specs=[
            row_spec,
            _resident((1, D_MODEL)),
            _resident((D_MODEL, 2 * D_FF)),
            _resident((D_FF, D_MODEL)),
            _resident((1, D_MODEL)),
        ],
        out_specs=row_spec,
        out_shape=jax.ShapeDtypeStruct((n, D_MODEL), F32),
        scratch_shapes=[
            pltpu.VMEM((tm, D_MODEL), BF16),
            pltpu.VMEM((tm, D_MODEL), F32),
        ],
        compiler_params=pltpu.CompilerParams(
            dimension_semantics=("arbitrary",), vmem_limit_bytes=VMEM_LIMIT_BYTES),
        name="ffn_final" if final_norm else "ffn",
    )(x, norm_g, w_gate_up, w_down, final_g)


def _mixer_kernel(x_ref, ha_ref, hb_ref, g_ref, win_ref, caw_ref, cbw_ref, cbb_ref, lng_ref,
                  lnb_ref, wao_ref, wbo_ref, wo_ref,
                  o_ref, na_ref, nb_ref,
                  h_scr, ab_scr, cv_scr, glu_scr, bc_scr, za_scr, zb_scr, m_scr,
                  *, seg_len, n_seg, tiles_per_seq):
    d = D_MODEL
    tm = seg_len * n_seg
    stride_a = HIST_A_PAD + seg_len
    stride_b = HIST_B_PAD + seg_len
    na = CONV_A_WIDTH - 1
    nb = CONV_B_WIDTH - 1
    n_lane_chunks = d // LANES
    lane_chunks_per_col_chunk = COL_CHUNK // LANES

    h_scr[...] = _rms_norm(x_ref[...], g_ref[...]).astype(BF16)

    for c in range(d // COL_CHUNK):
        lo = c * COL_CHUNK
        h = h_scr[...]
        ab_scr[:, lo:lo + COL_CHUNK] = _dot(h, win_ref[:, lo:lo + COL_CHUNK])
        cv = _dot(h, win_ref[:, d + lo:d + lo + COL_CHUNK]) * _dot(
            h, win_ref[:, 2 * d + lo:2 * d + lo + COL_CHUNK])
        glu = _dot(h, win_ref[:, 3 * d + lo:3 * d + lo + COL_CHUNK]) * jax.nn.sigmoid(
            _dot(h, win_ref[:, 4 * d + lo:4 * d + lo + COL_CHUNK]))
        for s in range(n_seg):
            rows = slice(s * seg_len, (s + 1) * seg_len)
            for j in range(lane_chunks_per_col_chunk):
                q = c * lane_chunks_per_col_chunk + j
                lanes = slice(j * LANES, (j + 1) * LANES)
                cv_scr[q, s * stride_a + HIST_A_PAD:(s + 1) * stride_a, :] = cv[rows, lanes]
                glu_scr[q, s * stride_b + HIST_B_PAD:(s + 1) * stride_b, :] = glu[rows, lanes]

    def load_hist():
        for s in range(n_seg):
            for q in range(n_lane_chunks):
                lanes = slice(q * LANES, (q + 1) * LANES)
                cv_scr[q, s * stride_a + HIST_A_PAD - na:s * stride_a + HIST_A_PAD, :] = (
                    ha_ref[s, :, lanes])
                glu_scr[q, s * stride_b + HIST_B_PAD - nb:s * stride_b + HIST_B_PAD, :] = (
                    hb_ref[s, :, lanes])

    if tiles_per_seq == 1:
        load_hist()
    else:
        pl.when(pl.program_id(0) % tiles_per_seq == 0)(load_hist)

    for s in range(n_seg):
        for q in range(n_lane_chunks):
            lanes = slice(q * LANES, (q + 1) * LANES)
            na_ref[s, :, lanes] = cv_scr[q, (s + 1) * stride_a - na:(s + 1) * stride_a, :]
            nb_ref[s, :, lanes] = glu_scr[q, (s + 1) * stride_b - nb:(s + 1) * stride_b, :]

    chunks_per_seg = seg_len // CONV_ROWS

    def conv_step(i, carry):
        seg = i // chunks_per_seg
        within = (i % chunks_per_seg) * CONV_ROWS
        row0 = pl.multiple_of(i * CONV_ROWS, CONV_ROWS)
        base_a = pl.multiple_of(seg * stride_a + within, 8)
        base_b = pl.multiple_of(seg * stride_b + within, 8)
        rows = pl.ds(row0, CONV_ROWS)

        for q in range(n_lane_chunks):
            lanes = slice(q * LANES, (q + 1) * LANES)
            acc = None
            for k in range(CONV_A_WIDTH):
                win = cv_scr[q, pl.ds(base_a + HIST_A_PAD - na + k, CONV_ROWS), :]
                term = caw_ref[q, k:k + 1, :] * win
                acc = term if acc is None else acc + term
            za_scr[rows, lanes] = (ab_scr[rows, lanes] * acc).astype(BF16)
            acc = None
            for k in range(CONV_B_WIDTH):
                win = glu_scr[q, pl.ds(base_b + HIST_B_PAD - nb + k, CONV_ROWS), :]
                term = cbw_ref[q, k:k + 1, :] * win
                acc = term if acc is None else acc + term
            bc_scr[rows, lanes] = acc + cbb_ref[:, lanes]

        for r in range(CONV_ROWS // LN_ROWS):
            sub = pl.ds(row0 + r * LN_ROWS, LN_ROWS)
            v = bc_scr[sub, :]
            mu = jnp.mean(v, axis=-1, keepdims=True)
            var = jnp.mean(jnp.square(v - mu), axis=-1, keepdims=True)
            y = (v - mu) * lax.rsqrt(var + EPS)
            y = y * lng_ref[...] + lnb_ref[...]
            zb_scr[sub, :] = jax.nn.silu(y).astype(BF16)
        return carry

    lax.fori_loop(0, tm // CONV_ROWS, conv_step, 0)

    if tiles_per_seq > 1:
        for q in range(n_lane_chunks):
            cv_scr[q, HIST_A_PAD - na:HIST_A_PAD, :] = cv_scr[q, stride_a - na:stride_a, :]
            glu_scr[q, HIST_B_PAD - nb:HIST_B_PAD, :] = glu_scr[q, stride_b - nb:stride_b, :]

    for c in range(d // COL_CHUNK):
        lo = c * COL_CHUNK
        h = h_scr[...]
        y_a = _dot(za_scr[...], wao_ref[:, lo:lo + COL_CHUNK])
        y_b = _dot(zb_scr[...], wbo_ref[:, lo:lo + COL_CHUNK])
        g_a = _dot(h, win_ref[:, 5 * d + lo:5 * d + lo + COL_CHUNK])
        g_b = _dot(h, win_ref[:, 6 * d + lo:6 * d + lo + COL_CHUNK])
        merged = jax.nn.sigmoid(g_a) * y_a + jax.nn.sigmoid(g_b) * y_b
        m_scr[:, lo:lo + COL_CHUNK] = merged.astype(BF16)

    for c in range(d // COL_CHUNK):
        lo = c * COL_CHUNK
        o_ref[:, lo:lo + COL_CHUNK] = x_ref[:, lo:lo + COL_CHUNK] + _dot(
            m_scr[...], wo_ref[:, lo:lo + COL_CHUNK])


def _lane_chunk_major(w):
    taps = w.shape[0]
    return w.reshape(taps, D_MODEL // LANES, LANES).transpose(1, 0, 2)


def _mixer(x, hist_a, hist_b, seq_len, norm_g, w_in, conv_a_w, conv_b_w, conv_b_bias, ln_g, ln_b,
           w_a_out, w_b_out, w_out):
    n = x.shape[0]
    batch = n // seq_len
    tm = min(TOKEN_TILE, n)
    seg_len = min(seq_len, tm)
    n_seg = tm // seg_len
    tiles_per_seq = seq_len // seg_len
    assert n % tm == 0 and tm % seg_len == 0 and seq_len % seg_len == 0
    assert seg_len % CONV_ROWS == 0 and seg_len >= CONV_B_WIDTH - 1
    assert n_seg == 1 or tiles_per_seq == 1
    na = CONV_A_WIDTH - 1
    nb = CONV_B_WIDTH - 1
    n_lane_chunks = D_MODEL // LANES
    conv_a_w = _lane_chunk_major(conv_a_w)
    conv_b_w = _lane_chunk_major(conv_b_w)

    row_spec = pl.BlockSpec((tm, D_MODEL), lambda t: (t, 0))

    def seq_block(rows):
        return pl.BlockSpec((n_seg, rows, D_MODEL), lambda t: (t // tiles_per_seq, 0, 0))

    kernel = functools.partial(_mixer_kernel, seg_len=seg_len, n_seg=n_seg,
                               tiles_per_seq=tiles_per_seq)
    return pl.pallas_call(
        kernel,
        grid=(n // tm,),
        in_specs=[
            row_spec,
            seq_block(na),
            seq_block(nb),
            _resident((1, D_MODEL)),
            _resident(w_in.shape),
            _resident(conv_a_w.shape),
            _resident(conv_b_w.shape),
            _resident((1, D_MODEL)),
            _resident((1, D_MODEL)),
            _resident((1, D_MODEL)),
            _resident(w_a_out.shape),
            _resident(w_b_out.shape),
            _resident(w_out.shape),
        ],
        out_specs=[row_spec, seq_block(na), seq_block(nb)],
        out_shape=[
            jax.ShapeDtypeStruct((n, D_MODEL), F32),
            jax.ShapeDtypeStruct((batch, na, D_MODEL), F32),
            jax.ShapeDtypeStruct((batch, nb, D_MODEL), F32),
        ],
        scratch_shapes=[
            pltpu.VMEM((tm, D_MODEL), BF16),
            pltpu.VMEM((tm, D_MODEL), F32),
            pltpu.VMEM((n_lane_chunks, n_seg * (HIST_A_PAD + seg_len), LANES), F32),
            pltpu.VMEM((n_lane_chunks, n_seg * (HIST_B_PAD + seg_len), LANES), F32),
            pltpu.VMEM((tm, D_MODEL), F32),
            pltpu.VMEM((tm, D_MODEL), BF16),
            pltpu.VMEM((tm, D_MODEL), BF16),
            pltpu.VMEM((tm, D_MODEL), BF16),
        ],
        compiler_params=pltpu.CompilerParams(
            dimension_semantics=("arbitrary",), vmem_limit_bytes=VMEM_LIMIT_BYTES),
        name="mixer",
    )(x, hist_a, hist_b, norm_g, w_in, conv_a_w, conv_b_w, conv_b_bias, ln_g, ln_b,
      w_a_out, w_b_out, w_out)


def kernel(x_prompt, x_sample, cache_conv_a, cache_conv_b, ffn1_norm, ffn1_w_gate_up, ffn1_w_down,
           mix_norm, w_in, conv_a_w, conv_b_w, conv_b_bias, conv_b_ln_g, conv_b_ln_b, w_a_out,
           w_b_out, w_out, ffn2_norm, ffn2_w_gate_up, ffn2_w_down, final_norm):
    depth = ffn1_norm.shape[0]
    batch_p, seq_p, _ = x_prompt.shape
    batch_s, seq_s, _ = x_sample.shape
    na = CONV_A_WIDTH - 1
    nb = CONV_B_WIDTH - 1

    xp = x_prompt.reshape(batch_p * seq_p, D_MODEL)
    xs = x_sample.reshape(batch_s * seq_s, D_MODEL)
    fin = final_norm.reshape(1, D_MODEL)
    zero_a = jnp.zeros((batch_p, na, D_MODEL), F32)
    zero_b = jnp.zeros((batch_p, nb, D_MODEL), F32)

    pa, pb, sa, sb = [], [], [], []
    for l in range(depth):
        last = l == depth - 1
        row = lambda v: v[l].reshape(1, D_MODEL)
        ffn1 = (row(ffn1_norm), ffn1_w_gate_up[l].astype(BF16), ffn1_w_down[l].astype(BF16), fin)
        ffn2 = (row(ffn2_norm), ffn2_w_gate_up[l].astype(BF16), ffn2_w_down[l].astype(BF16), fin)
        mix = (row(mix_norm), w_in[l].astype(BF16), conv_a_w[l], conv_b_w[l], row(conv_b_bias),
               row(conv_b_ln_g), row(conv_b_ln_b), w_a_out[l].astype(BF16),
               w_b_out[l].astype(BF16), w_out[l].astype(BF16))

        xp = _ffn(xp, *ffn1, final_norm=False)
        xp, hpa, hpb = _mixer(xp, zero_a, zero_b, seq_p, *mix)
        xp = _ffn(xp, *ffn2, final_norm=last)

        xs = _ffn(xs, *ffn1, final_norm=False)
        xs, hsa, hsb = _mixer(xs, cache_conv_a[l], cache_conv_b[l], seq_s, *mix)
        xs = _ffn(xs, *ffn2, final_norm=last)

        pa.append(hpa); pb.append(hpb); sa.append(hsa); sb.append(hsb)

    return (xp.reshape(x_prompt.shape), xs.reshape(x_sample.shape),
            jnp.stack(pa, axis=0), jnp.stack(pb, axis=0), jnp.stack(sa, axis=0), jnp.stack(sb, axis=0))
```

```python
import functools

import jax
import jax.numpy as jnp
from jax import lax
from jax.experimental import pallas as pl
from jax.experimental.pallas import tpu as pltpu

D_MODEL = 1024
D_FF = 2816
CONV_A_WIDTH = 3
CONV_B_WIDTH = 31
EPS = 1e-6

F32 = jnp.float32
BF16 = jnp.bfloat16

TOKEN_TILE = 512
COL_CHUNK = 256
LANES = 128
PACKED_ROW_TILE_COLS = 1024
CONV_ROWS = 64
LN_ROWS = 16
HIST_A_PAD = 8
HIST_B_PAD = 32
VMEM_LIMIT_BYTES = 52 * 1024 * 1024


def _rms_norm(x, g):
    y = x * lax.rsqrt(jnp.mean(x * x, axis=-1, keepdims=True) + EPS)
    return y * g


def _dot(a, b):
    return jnp.dot(a, b, preferred_element_type=F32)


def _mxu_weight(w):
    w = w.astype(BF16)
    if w.shape[1] % PACKED_ROW_TILE_COLS == 0:
        w = jnp.pad(w, ((0, 0), (0, LANES)))
    return w


def _resident(shape):
    return pl.BlockSpec(shape, lambda t: (0,) * len(shape), pipeline_mode=pl.Buffered(1))


def _ffn_kernel(x_ref, g_ref, wgu_ref, wd_ref, fin_ref, o_ref, h_scr, acc_scr, *, final_norm):
    h_scr[...] = _rms_norm(x_ref[...], g_ref[...]).astype(BF16)
    for j in range(D_FF // COL_CHUNK):
        lo = j * COL_CHUNK
        h = h_scr[...]
        gate = _dot(h, wgu_ref[:, lo:lo + COL_CHUNK])
        up = _dot(h, wgu_ref[:, D_FF + lo:D_FF + lo + COL_CHUNK])
        act = (jax.nn.silu(gate) * up).astype(BF16)
        part = _dot(act, wd_ref[lo:lo + COL_CHUNK, 0:D_MODEL])
        if j == 0:
            acc_scr[...] = part
        else:
            acc_scr[...] += part
    y = x_ref[...] + 0.5 * acc_scr[...]
    if final_norm:
        y = _rms_norm(y, fin_ref[...])
    o_ref[...] = y


def _ffn(x, norm_g, w_gate_up, w_down, final_g, *, final_norm):
    n = x.shape[0]
    tm = min(TOKEN_TILE, n)
    assert n % tm == 0
    row_spec = pl.BlockSpec((tm, D_MODEL), lambda t: (t, 0))
    return pl.pallas_call(
        functools.partial(_ffn_kernel, final_norm=final_norm),
        grid=(n // tm,),
        in_specs=[
            row_spec,
            _resident((1, D_MODEL)),
            _resident(w_gate_up.shape),
            _resident(w_down.shape),
            _resident((1, D_MODEL)),
        ],
        out_specs=row_spec,
        out_shape=jax.ShapeDtypeStruct((n, D_MODEL), F32),
        scratch_shapes=[
            pltpu.VMEM((tm, D_MODEL), BF16),
            pltpu.VMEM((tm, D_MODEL), F32),
        ],
        compiler_params=pltpu.CompilerParams(
            dimension_semantics=("arbitrary",), vmem_limit_bytes=VMEM_LIMIT_BYTES),
        name="ffn_final" if final_norm else "ffn",
    )(x, norm_g, w_gate_up, w_down, final_g)


def _mixer_kernel(x_ref, ha_ref, hb_ref, g_ref, win_ref, caw_ref, cbw_ref, cbb_ref, lng_ref,
                  lnb_ref, wao_ref, wbo_ref, wo_ref,
                  o_ref, na_ref, nb_ref,
                  h_scr, ab_scr, cv_scr, glu_scr, bc_scr, ga_scr, gb_scr, za_scr, zb_scr, m_scr,
                  *, seg_len, n_seg, tiles_per_seq):
    d = D_MODEL
    tm = seg_len * n_seg
    stride_a = HIST_A_PAD + seg_len
    stride_b = HIST_B_PAD + seg_len
    na = CONV_A_WIDTH - 1
    nb = CONV_B_WIDTH - 1
    n_lane_chunks = d // LANES
    n_col_chunks = d // COL_CHUNK
    lane_chunks_per_col_chunk = COL_CHUNK // LANES
    n_row_chunks = tm // CONV_ROWS

    def load_hist():
        for s in range(n_seg):
            for q in range(n_lane_chunks):
                lanes = slice(q * LANES, (q + 1) * LANES)
                cv_scr[q, s * stride_a + HIST_A_PAD - na:s * stride_a + HIST_A_PAD, :] = (
                    ha_ref[s, :, lanes])
                glu_scr[q, s * stride_b + HIST_B_PAD - nb:s * stride_b + HIST_B_PAD, :] = (
                    hb_ref[s, :, lanes])

    if tiles_per_seq == 1:
        load_hist()
    else:
        pl.when(pl.program_id(0) % tiles_per_seq == 0)(load_hist)

    h_scr[...] = _rms_norm(x_ref[...], g_ref[...]).astype(BF16)

    def proj(block, c):
        lo = block * d + c * COL_CHUNK
        return _dot(h_scr[...], win_ref[:, lo:lo + COL_CHUNK])

    def scatter_segments(dst, pad, stride, c, val):
        for s in range(n_seg):
            rows = slice(s * seg_len, (s + 1) * seg_len)
            for j in range(lane_chunks_per_col_chunk):
                q = c * lane_chunks_per_col_chunk + j
                dst[q, s * stride + pad:(s + 1) * stride, :] = val[rows, j * LANES:(j + 1) * LANES]

    def seg_base(rc, stride):
        seg, within = divmod(rc * CONV_ROWS, seg_len)
        return seg * stride + within

    def conv_a(q, rc):
        rows = slice(rc * CONV_ROWS, (rc + 1) * CONV_ROWS)
        lanes = slice(q * LANES, (q + 1) * LANES)
        base = seg_base(rc, stride_a) + HIST_A_PAD - na
        acc = None
        for k in range(CONV_A_WIDTH):
            term = caw_ref[q, k:k + 1, :] * cv_scr[q, base + k:base + k + CONV_ROWS, :]
            acc = term if acc is None else acc + term
        za_scr[rows, lanes] = (ab_scr[rows, lanes] * acc).astype(BF16)

    def conv_b(q, rc):
        rows = slice(rc * CONV_ROWS, (rc + 1) * CONV_ROWS)
        lanes = slice(q * LANES, (q + 1) * LANES)
        base = seg_base(rc, stride_b) + HIST_B_PAD - nb
        acc = None
        for k in range(CONV_B_WIDTH):
            term = cbw_ref[q, k:k + 1, :] * glu_scr[q, base + k:base + k + CONV_ROWS, :]
            acc = term if acc is None else acc + term
        bc_scr[rows, lanes] = acc + cbb_ref[:, lanes]

    for c in range(n_col_chunks):
        glu = proj(3, c) * jax.nn.sigmoid(proj(4, c))
        scatter_segments(glu_scr, HIST_B_PAD, stride_b, c, glu)

    for c in range(n_col_chunks):
        cols = slice(c * COL_CHUNK, (c + 1) * COL_CHUNK)
        ab_scr[:, cols] = proj(0, c)
        scatter_segments(cv_scr, HIST_A_PAD, stride_a, c, proj(1, c) * proj(2, c))
        ga_scr[:, cols] = jax.nn.sigmoid(proj(5, c))
        gb_scr[:, cols] = jax.nn.sigmoid(proj(6, c))
        for j in range(lane_chunks_per_col_chunk):
            q = c * lane_chunks_per_col_chunk + j
            for rc in range(n_row_chunks):
                conv_b(q, rc)

    for s in range(n_seg):
        for q in range(n_lane_chunks):
            lanes = slice(q * LANES, (q + 1) * LANES)
            na_ref[s, :, lanes] = cv_scr[q, (s + 1) * stride_a - na:(s + 1) * stride_a, :]
            nb_ref[s, :, lanes] = glu_scr[q, (s + 1) * stride_b - nb:(s + 1) * stride_b, :]

    for q in range(n_lane_chunks):
        for rc in range(n_row_chunks):
            conv_a(q, rc)

    for r in range(tm // LN_ROWS):
        sub = slice(r * LN_ROWS, (r + 1) * LN_ROWS)
        v = bc_scr[sub, :]
        mu = jnp.mean(v, axis=-1, keepdims=True)
        var = jnp.mean(jnp.square(v - mu), axis=-1, keepdims=True)
        y = (v - mu) * lax.rsqrt(var + EPS)
        y = y * lng_ref[...] + lnb_ref[...]
        zb_scr[sub, :] = jax.nn.silu(y).astype(BF16)

    if tiles_per_seq > 1:
        for q in range(n_lane_chunks):
            cv_scr[q, HIST_A_PAD - na:HIST_A_PAD, :] = cv_scr[q, stride_a - na:stride_a, :]
            glu_scr[q, HIST_B_PAD - nb:HIST_B_PAD, :] = glu_scr[q, stride_b - nb:stride_b, :]

    for c in range(n_col_chunks):
        cols = slice(c * COL_CHUNK, (c + 1) * COL_CHUNK)
        y_a = _dot(za_scr[...], wao_ref[:, cols])
        y_b = _dot(zb_scr[...], wbo_ref[:, cols])
        m_scr[:, cols] = (ga_scr[:, cols] * y_a + gb_scr[:, cols] * y_b).astype(BF16)

    for c in range(n_col_chunks):
        cols = slice(c * COL_CHUNK, (c + 1) * COL_CHUNK)
        o_ref[:, cols] = x_ref[:, cols] + _dot(m_scr[...], wo_ref[:, cols])


def _lane_chunk_major(w):
    taps = w.shape[0]
    return w.reshape(taps, D_MODEL // LANES, LANES).transpose(1, 0, 2)


def _mixer(x, hist_a, hist_b, seq_len, norm_g, w_in, conv_a_w, conv_b_w, conv_b_bias, ln_g, ln_b,
           w_a_out, w_b_out, w_out):
    n = x.shape[0]
    batch = n // seq_len
    tm = min(TOKEN_TILE, n)
    seg_len = min(seq_len, tm)
    n_seg = tm // seg_len
    tiles_per_seq = seq_len // seg_len
    assert n % tm == 0 and tm % seg_len == 0 and seq_len % seg_len == 0
    assert seg_len % CONV_ROWS == 0 and seg_len >= CONV_B_WIDTH - 1
    assert n_seg == 1 or tiles_per_seq == 1
    na = CONV_A_WIDTH - 1
    nb = CONV_B_WIDTH - 1
    n_lane_chunks = D_MODEL // LANES
    conv_a_w = _lane_chunk_major(conv_a_w)
    conv_b_w = _lane_chunk_major(conv_b_w)

    row_spec = pl.BlockSpec((tm, D_MODEL), lambda t: (t, 0))

    def seq_block(rows):
        return pl.BlockSpec((n_seg, rows, D_MODEL), lambda t: (t // tiles_per_seq, 0, 0))

    kernel = functools.partial(_mixer_kernel, seg_len=seg_len, n_seg=n_seg,
                               tiles_per_seq=tiles_per_seq)
    return pl.pallas_call(
        kernel,
        grid=(n // tm,),
        in_specs=[
            row_spec,
            seq_block(na),
            seq_block(nb),
            _resident((1, D_MODEL)),
            _resident(w_in.shape),
            _resident(conv_a_w.shape),
            _resident(conv_b_w.shape),
            _resident((1, D_MODEL)),
            _resident((1, D_MODEL)),
            _resident((1, D_MODEL)),
            _resident(w_a_out.shape),
            _resident(w_b_out.shape),
            _resident(w_out.shape),
        ],
        out_specs=[row_spec, seq_block(na), seq_block(nb)],
        out_shape=[
            jax.ShapeDtypeStruct((n, D_MODEL), F32),
            jax.ShapeDtypeStruct((batch, na, D_MODEL), F32),
            jax.ShapeDtypeStruct((batch, nb, D_MODEL), F32),
        ],
        scratch_shapes=[
            pltpu.VMEM((tm, D_MODEL), BF16),
            pltpu.VMEM((tm, D_MODEL), F32),
            pltpu.VMEM((n_lane_chunks, n_seg * (HIST_A_PAD + seg_len), LANES), F32),
            pltpu.VMEM((n_lane_chunks, n_seg * (HIST_B_PAD + seg_len), LANES), F32),
            pltpu.VMEM((tm, D_MODEL), F32),
            pltpu.VMEM((tm, D_MODEL), F32),
            pltpu.VMEM((tm, D_MODEL), F32),
            pltpu.VMEM((tm, D_MODEL), BF16),
            pltpu.VMEM((tm, D_MODEL), BF16),
            pltpu.VMEM((tm, D_MODEL), BF16),
        ],
        compiler_params=pltpu.CompilerParams(
            dimension_semantics=("arbitrary",), vmem_limit_bytes=VMEM_LIMIT_BYTES),
        name="mixer",
    )(x, hist_a, hist_b, norm_g, w_in, conv_a_w, conv_b_w, conv_b_bias, ln_g, ln_b,
      w_a_out, w_b_out, w_out)


def kernel(x_prompt, x_sample, cache_conv_a, cache_conv_b, ffn1_norm, ffn1_w_gate_up, ffn1_w_down,
           mix_norm, w_in, conv_a_w, conv_b_w, conv_b_bias, conv_b_ln_g, conv_b_ln_b, w_a_out,
           w_b_out, w_out, ffn2_norm, ffn2_w_gate_up, ffn2_w_down, final_norm):
    depth = ffn1_norm.shape[0]
    batch_p, seq_p, _ = x_prompt.shape
    batch_s, seq_s, _ = x_sample.shape
    na = CONV_A_WIDTH - 1
    nb = CONV_B_WIDTH - 1

    xp = x_prompt.reshape(batch_p * seq_p, D_MODEL)
    xs = x_sample.reshape(batch_s * seq_s, D_MODEL)
    fin = final_norm.reshape(1, D_MODEL)
    zero_a = jnp.zeros((batch_p, na, D_MODEL), F32)
    zero_b = jnp.zeros((batch_p, nb, D_MODEL), F32)

    pa, pb, sa, sb = [], [], [], []
    for l in range(depth):
        last = l == depth - 1
        row = lambda v: v[l].reshape(1, D_MODEL)
        ffn1 = (row(ffn1_norm), _mxu_weight(ffn1_w_gate_up[l]), _mxu_weight(ffn1_w_down[l]), fin)
        ffn2 = (row(ffn2_norm), _mxu_weight(ffn2_w_gate_up[l]), _mxu_weight(ffn2_w_down[l]), fin)
        mix = (row(mix_norm), _mxu_weight(w_in[l]), conv_a_w[l], conv_b_w[l], row(conv_b_bias),
               row(conv_b_ln_g), row(conv_b_ln_b), _mxu_weight(w_a_out[l]),
               _mxu_weight(w_b_out[l]), _mxu_weight(w_out[l]))

        xp = _ffn(xp, *ffn1, final_norm=False)
        xp, hpa, hpb = _mixer(xp, zero_a, zero_b, seq_p, *mix)
        xp = _ffn(xp, *ffn2, final_norm=last)

        xs = _ffn(xs, *ffn1, final_norm=False)
        xs, hsa, hsb = _mixer(xs, cache_conv_a[l], cache_conv_b[l], seq_s, *mix)
        xs = _ffn(xs, *ffn2, final_norm=last)

        pa.append(hpa); pb.append(hpb); sa.append(hsa); sb.append(hsb)

    return (xp.reshape(x_prompt.shape), xs.reshape(x_sample.shape),
            jnp.stack(pa, axis=0), jnp.stack(pb, axis=0), jnp.stack(sa, axis=0), jnp.stack(sb, axis=0))
```
